```python
import jax, jax.numpy as jnp
from jax import lax
import numpy as np

D_MODEL = 4096
BATCH = 4
SEQ = 2048
DEPTH = 2
DEC_BATCH = 32
DEC_SEQ = 1
PAST_LEN = 16384
PAGE_SIZE = 128

N_MIXERS = 2
N_RET_LAYERS = (DEPTH + 1) // 2
N_ATTN_LAYERS = DEPTH // 2
RET_HEADS = 16
RET_DK = D_MODEL // RET_HEADS
RET_DV = 2 * D_MODEL // RET_HEADS
RET_CHUNK = 128
ATT_HEAD_DIM = 64
ATT_Q_HEADS = D_MODEL // ATT_HEAD_DIM
ATT_KV_HEADS = 8
ATT_GROUP = ATT_Q_HEADS // ATT_KV_HEADS
WINDOW = 128
ATT_BLOCK = 128
D_FF = 4 * D_MODEL
PLE_DIM = 256
RMS_EPS = 1e-6
GN_EPS = 1e-6

kernel_name = 'retention_swa_sink_hybrid_step'


def rms_norm(x, gain):
    xf = x.astype(jnp.float32)
    y = xf * lax.rsqrt(jnp.mean(xf * xf, axis=-1, keepdims=True) + RMS_EPS)
    return (y * gain.astype(jnp.float32)).astype(x.dtype)


def retention_log_decay():
    return jnp.log1p(-jnp.exp2(-5.0 - jnp.arange(RET_HEADS, dtype=jnp.float32)))


def alibi_slopes():
    h = jnp.arange(1, ATT_Q_HEADS + 1, dtype=jnp.float32)
    return jnp.exp2(-8.0 * h / ATT_Q_HEADS).reshape(ATT_KV_HEADS, ATT_GROUP)


def retention_chunk(state, q, k, v, log_g):
    L = q.shape[1]
    idx = jnp.arange(L, dtype=jnp.float32)
    diff = idx[:, None] - idx[None, :]
    causal = diff >= 0
    decay = jnp.where(causal[None], jnp.exp(jnp.where(causal, diff, 0.0)[None] * log_g[:, None, None]), 0.0)
    scores = jnp.einsum('blhd,bmhd->bhlm', q, k) * decay[None]
    out = jnp.einsum('bhlm,bmhe->blhe', scores, v)
    q_decay = jnp.exp((idx + 1.0)[:, None] * log_g[None, :])
    out = out + jnp.einsum('blhd,bhde->blhe', q * q_decay[None, :, :, None], state)
    k_decay = jnp.exp((L - 1.0 - idx)[:, None] * log_g[None, :])
    new_state = jnp.exp(L * log_g)[None, :, None, None] * state + jnp.einsum(
        'blhd,blhe->bhde', k * k_decay[None, :, :, None], v)
    return new_state, out


def retention_mixer(h, w_in, gn_gain, w_out, state):
    B, L, _ = h.shape
    qk_w = RET_HEADS * RET_DK
    v_w = RET_HEADS * RET_DV
    proj = h @ w_in
    q, k, v, g = jnp.split(proj, [qk_w, 2 * qk_w, 2 * qk_w + v_w], axis=-1)
    q = q.reshape(B, L, RET_HEADS, RET_DK).astype(jnp.float32)
    k = k.reshape(B, L, RET_HEADS, RET_DK).astype(jnp.float32) * (RET_DK ** -0.5)
    v = v.reshape(B, L, RET_HEADS, RET_DV).astype(jnp.float32)
    log_g = retention_log_decay()
    if state is None:
        nc = L // RET_CHUNK
        def to_chunks(t):
            return t.reshape(B, nc, RET_CHUNK, *t.shape[2:]).swapaxes(0, 1)
        s0 = jnp.zeros((B, RET_HEADS, RET_DK, RET_DV), jnp.float32)
        def step(s, qkv):
            return retention_chunk(s, qkv[0], qkv[1], qkv[2], log_g)
        s_new, o = lax.scan(step, s0, (to_chunks(q), to_chunks(k), to_chunks(v)))
        o = o.swapaxes(0, 1).reshape(B, L, RET_HEADS, RET_DV)
    else:
        s_new, o = retention_chunk(state.astype(jnp.float32), q, k, v, log_g)
    mu = jnp.mean(o, axis=-1, keepdims=True)
    var = jnp.mean(jnp.square(o - mu), axis=-1, keepdims=True)
    o = ((o - mu) * lax.rsqrt(var + GN_EPS)).reshape(B, L, v_w) * gn_gain.astype(jnp.float32)
    y = (jax.nn.silu(g.astype(jnp.float32)) * o).astype(h.dtype) @ w_out
    return y, s_new.astype(h.dtype)


def attn_project(h, w_qkv, b_qkv, q_norm, k_norm):
    B, L, _ = h.shape
    q_w = ATT_Q_HEADS * ATT_HEAD_DIM
    kv_w = ATT_KV_HEADS * ATT_HEAD_DIM
    proj = h @ w_qkv + b_qkv
    q, k, v = jnp.split(proj, [q_w, q_w + kv_w], axis=-1)
    q = rms_norm(q.reshape(B, L, ATT_KV_HEADS, ATT_GROUP, ATT_HEAD_DIM), q_norm) * (ATT_HEAD_DIM ** -0.5)
    k = rms_norm(k.reshape(B, L, ATT_KV_HEADS, ATT_HEAD_DIM), k_norm)
    v = v.reshape(B, L, ATT_KV_HEADS, ATT_HEAD_DIM)
    return q, k, v


def sink_softmax(scores, dist, valid, slopes, sinks):
    logits = scores - slopes[:, :, None, None] * dist
    logits = jnp.where(valid, logits, -jnp.inf)
    sink = sinks.reshape(ATT_KV_HEADS, ATT_GROUP)[:, :, None, None].astype(jnp.float32)
    m = jnp.maximum(jnp.max(logits, axis=-1, keepdims=True), sink)
    e = jnp.exp(logits - m)
    denom = jnp.sum(e, axis=-1, keepdims=True) + jnp.exp(sink - m)
    return e / denom


def swa_prompt(q, k, v, slopes, sinks):
    B, S = q.shape[:2]
    nb = S // ATT_BLOCK
    qb = q.reshape(B, nb, ATT_BLOCK, ATT_KV_HEADS, ATT_GROUP, ATT_HEAD_DIM)
    kb = k.reshape(B, nb, ATT_BLOCK, ATT_KV_HEADS, ATT_HEAD_DIM)
    vb = v.reshape(B, nb, ATT_BLOCK, ATT_KV_HEADS, ATT_HEAD_DIM)
    def prev(t):
        return jnp.concatenate([jnp.zeros_like(t[:, :1]), t[:, :-1]], axis=1)
    kk = jnp.concatenate([prev(kb), kb], axis=2)
    vv = jnp.concatenate([prev(vb), vb], axis=2)
    scores = jnp.einsum('bnqhgd,bnkhd->bnhgqk', qb, kk).astype(jnp.float32)
    qpos = ATT_BLOCK + jnp.arange(ATT_BLOCK)
    kpos = jnp.arange(2 * ATT_BLOCK)
    dist = qpos[:, None] - kpos[None, :]
    in_window = (dist >= 0) & (dist <= WINDOW)
    has_prev = (jnp.arange(nb) > 0)[:, None, None] | (kpos >= ATT_BLOCK)[None, None, :]
    valid = (in_window[None] & has_prev)[:, None, None]
    probs = sink_softmax(scores, dist.astype(jnp.float32), valid, slopes, sinks)
    out = jnp.einsum('bnhgqk,bnkhd->bnqhgd', probs.astype(v.dtype), vv)
    return out.reshape(B, S, ATT_Q_HEADS * ATT_HEAD_DIM)


def swa_sample(q, k, v, cache_k, cache_v, slopes, sinks):
    B, L = q.shape[:2]
    wb = cache_k.shape[1]
    kk = jnp.concatenate([cache_k, k], axis=1)
    vv = jnp.concatenate([cache_v, v], axis=1)
    qpos = PAST_LEN + jnp.arange(L)
    kpos = jnp.concatenate([PAST_LEN - wb + jnp.arange(wb), PAST_LEN + jnp.arange(L)])
    dist = qpos[:, None] - kpos[None, :]
    valid = (dist >= 0) & (dist <= WINDOW)
    scores = jnp.einsum('blhgd,bkhd->bhglk', q, kk).astype(jnp.float32)
    probs = sink_softmax(scores, dist.astype(jnp.float32), valid, slopes, sinks)
    out = jnp.einsum('bhglk,bkhd->blhgd', probs.astype(v.dtype), vv)
    return out.reshape(B, L, ATT_Q_HEADS * ATT_HEAD_DIM), kk[:, -wb:], vv[:, -wb:]


def channel_mixer(h, w_up, w_down):
    return jnp.square(jax.nn.relu(h @ w_up)) @ w_down


def per_layer_embed(r, p_i, gain, w_gate, w_proj):
    h = rms_norm(r, gain)
    gate = jax.nn.sigmoid((h @ w_gate).astype(jnp.float32))
    return r + (gate * (p_i @ w_proj).astype(jnp.float32)).astype(r.dtype)


def setup_inputs(seed: int = 0) -> dict:
    key = jax.random.key(seed)
    ks = jax.random.split(key, 24)
    f32 = jnp.float32
    w_buf = min(WINDOW, PAST_LEN)
    ret_in = RET_HEADS * (2 * RET_DK + 2 * RET_DV)
    att_in = (ATT_Q_HEADS + 2 * ATT_KV_HEADS) * ATT_HEAD_DIM
    def nrm(k, shape, scale=1.0):
        return jax.random.normal(k, shape, f32) * scale
    def gain(k, shape):
        return 1.0 + 0.02 * jax.random.normal(k, shape, f32)
    return {
        'x_prompt': nrm(ks[0], (BATCH, SEQ, D_MODEL)),
        'x_sample': nrm(ks[1], (DEC_BATCH, DEC_SEQ, D_MODEL)),
        'p_prompt': nrm(ks[2], (DEPTH, BATCH, SEQ, PLE_DIM)),
        'p_sample': nrm(ks[3], (DEPTH, DEC_BATCH, DEC_SEQ, PLE_DIM)),
        'state_ret': nrm(ks[4], (N_RET_LAYERS, DEC_BATCH, RET_HEADS, RET_DK, RET_DV), 0.5),
        'cache_k': nrm(ks[5], (N_ATTN_LAYERS, DEC_BATCH, w_buf, ATT_KV_HEADS, ATT_HEAD_DIM)),
        'cache_v': nrm(ks[6], (N_ATTN_LAYERS, DEC_BATCH, w_buf, ATT_KV_HEADS, ATT_HEAD_DIM)),
        'norm_mix': gain(ks[7], (DEPTH, D_MODEL)),
        'norm_mlp': gain(ks[8], (DEPTH, D_MODEL)),
        'norm_ple': gain(ks[9], (DEPTH, D_MODEL)),
        'ret_w_in': nrm(ks[10], (N_RET_LAYERS, D_MODEL, ret_in), D_MODEL ** -0.5),
        'ret_gn_gain': gain(ks[11], (N_RET_LAYERS, RET_HEADS * RET_DV)),
        'ret_w_out': nrm(ks[12], (N_RET_LAYERS, RET_HEADS * RET_DV, D_MODEL), (RET_HEADS * RET_DV) ** -0.5),
        'attn_w_qkv': nrm(ks[13], (N_ATTN_LAYERS, D_MODEL, att_in), D_MODEL ** -0.5),
        'attn_b_qkv': nrm(ks[14], (N_ATTN_LAYERS, att_in), 0.02),
        'attn_q_norm': gain(ks[15], (N_ATTN_LAYERS, ATT_HEAD_DIM)),
        'attn_k_norm': gain(ks[16], (N_ATTN_LAYERS, ATT_HEAD_DIM)),
        'attn_sinks': nrm(ks[17], (N_ATTN_LAYERS, ATT_Q_HEADS), 0.5),
        'attn_w_out': nrm(ks[18], (N_ATTN_LAYERS, ATT_Q_HEADS * ATT_HEAD_DIM, D_MODEL), (ATT_Q_HEADS * ATT_HEAD_DIM) ** -0.5),
        'mlp_w_up': nrm(ks[19], (DEPTH, D_MODEL, D_FF), D_MODEL ** -0.5),
        'mlp_w_down': nrm(ks[20], (DEPTH, D_FF, D_MODEL), D_FF ** -0.5),
        'ple_w_gate': nrm(ks[21], (DEPTH, D_MODEL, D_MODEL), D_MODEL ** -0.5),
        'ple_w_proj': nrm(ks[22], (DEPTH, PLE_DIM, D_MODEL), PLE_DIM ** -0.5),
    }


def reference(x_prompt, x_sample, p_prompt, p_sample, state_ret, cache_k, cache_v,
              norm_mix, norm_mlp, norm_ple, ret_w_in, ret_gn_gain, ret_w_out,
              attn_w_qkv, attn_b_qkv, attn_q_norm, attn_k_norm, attn_sinks, attn_w_out,
              mlp_w_up, mlp_w_down, ple_w_gate, ple_w_proj):
    slopes = alibi_slopes()
    rp, rs = x_prompt, x_sample
    ret_p, ret_s, kbuf_p, vbuf_p, kbuf_s, vbuf_s = [], [], [], [], [], []
    for i in range(DEPTH):
        j = i // N_MIXERS
        hp = rms_norm(rp, norm_mix[i])
        hs = rms_norm(rs, norm_mix[i])
        if i % N_MIXERS == 0:
            yp, sp = retention_mixer(hp, ret_w_in[j], ret_gn_gain[j], ret_w_out[j], None)
            ys, ss = retention_mixer(hs, ret_w_in[j], ret_gn_gain[j], ret_w_out[j], state_ret[j])
            ret_p.append(sp)
            ret_s.append(ss)
        else:
            qp, kp, vp = attn_project(hp, attn_w_qkv[j], attn_b_qkv[j], attn_q_norm[j], attn_k_norm[j])
            yp = swa_prompt(qp, kp, vp, slopes, attn_sinks[j]) @ attn_w_out[j]
            kbuf_p.append(kp[:, -WINDOW:])
            vbuf_p.append(vp[:, -WINDOW:])
            qs, ks_, vs_ = attn_project(hs, attn_w_qkv[j], attn_b_qkv[j], attn_q_norm[j], attn_k_norm[j])
            os_, kb, vb = swa_sample(qs, ks_, vs_, cache_k[j], cache_v[j], slopes, attn_sinks[j])
            ys = os_ @ attn_w_out[j]
            kbuf_s.append(kb)
            vbuf_s.append(vb)
        rp = rp + yp
        rs = rs + ys
        rp = rp + channel_mixer(rms_norm(rp, norm_mlp[i]), mlp_w_up[i], mlp_w_down[i])
        rs = rs + channel_mixer(rms_norm(rs, norm_mlp[i]), mlp_w_up[i], mlp_w_down[i])
        rp = per_layer_embed(rp, p_prompt[i], norm_ple[i], ple_w_gate[i], ple_w_proj[i])
        rs = per_layer_embed(rs, p_sample[i], norm_ple[i], ple_w_gate[i], ple_w_proj[i])
    return (rp, rs, jnp.stack(ret_p), jnp.stack(ret_s), jnp.stack(kbuf_p), jnp.stack(vbuf_p), jnp.stack(kbuf_s), jnp.stack(vbuf_s))
```

```python
import functools

import jax
import jax.numpy as jnp
import numpy as np
from jax import lax
from jax.experimental import pallas as pl
from jax.experimental.pallas import tpu as pltpu

D_MODEL = 4096
N_MIXERS = 2
RET_HEADS = 16
RET_DK = D_MODEL // RET_HEADS
RET_DV = 2 * D_MODEL // RET_HEADS
RET_CHUNK = 128
ATT_HEAD_DIM = 64
ATT_Q_HEADS = D_MODEL // ATT_HEAD_DIM
ATT_KV_HEADS = 8
ATT_GROUP = ATT_Q_HEADS // ATT_KV_HEADS
WINDOW = 128
ATT_BLOCK = 128
PAST_LEN = 16384
RMS_EPS = 1e-6
GN_EPS = 1e-6

LANES = 128
V7X_VMEM_LIMIT_BYTES = 56 * 1024 * 1024

BF16 = jnp.bfloat16
F32 = jnp.float32


def _params(n_grid_axes):
    return pltpu.CompilerParams(
        dimension_semantics=("arbitrary",) * n_grid_axes,
        vmem_limit_bytes=V7X_VMEM_LIMIT_BYTES,
    )


def _rmsnorm_kernel(x_ref, g_ref, o_ref):
    x = x_ref[...]
    y = x * lax.rsqrt(jnp.mean(x * x, axis=-1, keepdims=True) + RMS_EPS)
    o_ref[...] = (y * g_ref[...]).astype(o_ref.dtype)


def _rmsnorm(x, gain):
    m, d = x.shape
    tm = min(m, 256)
    return pl.pallas_call(
        _rmsnorm_kernel,
        grid=(m // tm,),
        in_specs=[pl.BlockSpec((tm, d), lambda i: (i, 0)), pl.BlockSpec((1, d), lambda i: (0, 0))],
        out_specs=pl.BlockSpec((tm, d), lambda i: (i, 0)),
        out_shape=jax.ShapeDtypeStruct((m, d), BF16),
        compiler_params=_params(1),
        name="rmsnorm",
    )(x, gain.reshape(1, d))


def _mm_tiles(m, k, n):
    if m <= 64:
        return m, min(n, 1024), min(k, 2048)
    if k <= 4096:
        return min(m, 1024), min(n, 512), k
    return min(m, 1024), min(n, 1024), 2048


def _mm_kernel(*refs, nk, epilogue):
    x_ref, w_ref = refs[0], refs[1]
    extra = refs[2:-1] if nk == 1 else refs[2:-2]
    o_ref = refs[-1] if nk == 1 else refs[-2]

    part = jnp.dot(x_ref[...], w_ref[...].astype(BF16), preferred_element_type=F32)

    def finish(acc):
        if epilogue == "plain":
            res = acc
        elif epilogue == "bias":
            res = acc + extra[0][...]
        elif epilogue == "residual":
            res = extra[0][...] + acc
        elif epilogue == "relu2":
            res = jnp.square(jnp.maximum(acc, 0.0))
        elif epilogue == "ple":
            r_ref, p_ref, wp_ref = extra
            proj = jnp.dot(p_ref[...].astype(BF16), wp_ref[...].astype(BF16), preferred_element_type=F32)
            res = r_ref[...] + jax.nn.sigmoid(acc) * proj
        else:
            raise ValueError(epilogue)
        o_ref[...] = res.astype(o_ref.dtype)

    if nk == 1:
        finish(part)
        return

    acc_ref = refs[-1]
    k = pl.program_id(2)

    @pl.when(k == 0)
    def _():
        acc_ref[...] = part

    @pl.when(jnp.logical_and(k > 0, k < nk - 1))
    def _():
        acc_ref[...] += part

    @pl.when(k == nk - 1)
    def _():
        finish(acc_ref[...] + part)


def _matmul(x, w, layer, *, epilogue="plain", out_dtype=F32, bias=None, residual=None, ple=None):
    m, k = x.shape
    n = w.shape[2]
    tm, tn, tk = _mm_tiles(m, k, n)
    nk = k // tk
    in_specs = [
        pl.BlockSpec((tm, tk), lambda j, i, kk: (i, kk)),
        pl.BlockSpec((None, tk, tn), lambda j, i, kk: (layer, kk, j)),
    ]
    args = [x, w]
    if epilogue == "bias":
        in_specs.append(pl.BlockSpec((1, tn), lambda j, i, kk: (0, j)))
        args.append(bias.reshape(1, n))
    elif epilogue == "residual":
        in_specs.append(pl.BlockSpec((tm, tn), lambda j, i, kk: (i, j)))
        args.append(residual)
    elif epilogue == "ple":
        p, wp = ple
        pd = p.shape[2]
        in_specs += [
            pl.BlockSpec((tm, tn), lambda j, i, kk: (i, j)),
            pl.BlockSpec((None, tm, pd), lambda j, i, kk: (layer, i, 0)),
            pl.BlockSpec((None, pd, tn), lambda j, i, kk: (layer, 0, j)),
        ]
        args += [residual, p, wp]
    scratch = [] if nk == 1 else [pltpu.VMEM((tm, tn), F32)]
    return pl.pallas_call(
        functools.partial(_mm_kernel, nk=nk, epilogue=epilogue),
        grid=(n // tn, m // tm, nk),
        in_specs=in_specs,
        out_specs=pl.BlockSpec((tm, tn), lambda j, i, kk: (i, j)),
        out_shape=jax.ShapeDtypeStruct((m, n), out_dtype),
        scratch_shapes=scratch,
        compiler_params=_params(3),
        name="matmul_" + epilogue,
    )(*args)


def _retention_log_decay():
    return jnp.log1p(-jnp.exp2(-5.0 - jnp.arange(RET_HEADS, dtype=F32)))


def _group_norm_gate(o, g, gn):
    mu = jnp.mean(o, axis=-1, keepdims=True)
    d = o - mu
    var = jnp.mean(d * d, axis=-1, keepdims=True)
    return jax.nn.silu(g) * (d * lax.rsqrt(var + GN_EPS) * gn)


def _retention_kernel(sdec_ref, q_ref, k_ref, v_ref, g_ref, gn_ref, dec_ref, qd_ref, kd_ref, o_ref, s_ref):
    h = pl.program_id(1)

    @pl.when(pl.program_id(2) == 0)
    def _():
        s_ref[...] = jnp.zeros_like(s_ref)

    q = q_ref[...]
    k = k_ref[...] * (RET_DK ** -0.5)
    vb = v_ref[...].astype(BF16)
    state = s_ref[0, 0]
    scores = lax.dot_general(q.astype(BF16), k.astype(BF16), (((1,), (1,)), ((), ())),
                             preferred_element_type=F32) * dec_ref[0]
    o = jnp.dot(scores.astype(BF16), vb, preferred_element_type=F32)
    o = o + jnp.dot((q * qd_ref[0]).astype(BF16), state.astype(BF16), preferred_element_type=F32)
    kv = lax.dot_general((k * kd_ref[0]).astype(BF16), vb, (((0,), (0,)), ((), ())),
                         preferred_element_type=F32)
    s_ref[0, 0] = sdec_ref[h] * state + kv
    o_ref[...] = _group_norm_gate(o, g_ref[...], gn_ref[...]).astype(o_ref.dtype)


def _retention_prompt(proj, gn_gain, batch, seq):
    nc = seq // RET_CHUNK
    log_g = _retention_log_decay()
    idx = jnp.arange(RET_CHUNK, dtype=F32)
    diff = idx[:, None] - idx[None, :]
    causal = diff >= 0
    decay = jnp.where(causal[None], jnp.exp(jnp.where(causal, diff, 0.0)[None] * log_g[:, None, None]), 0.0)
    q_decay = jnp.exp((idx + 1.0)[None, :] * log_g[:, None])[:, :, None]
    k_decay = jnp.exp((RET_CHUNK - 1.0 - idx)[None, :] * log_g[:, None])[:, :, None]
    s_decay = jnp.exp(RET_CHUNK * log_g)
    kq_blocks = RET_HEADS
    v_off = 2 * RET_HEADS * RET_DK // RET_DV
    g_off = v_off + RET_HEADS
    row = lambda b, h, c, *_: b * nc + c
    grid_spec = pltpu.PrefetchScalarGridSpec(
        num_scalar_prefetch=1,
        grid=(batch, RET_HEADS, nc),
        in_specs=[
            pl.BlockSpec((RET_CHUNK, RET_DK), lambda b, h, c, s: (row(b, h, c), h)),
            pl.BlockSpec((RET_CHUNK, RET_DK), lambda b, h, c, s: (row(b, h, c), kq_blocks + h)),
            pl.BlockSpec((RET_CHUNK, RET_DV), lambda b, h, c, s: (row(b, h, c), v_off + h)),
            pl.BlockSpec((RET_CHUNK, RET_DV), lambda b, h, c, s: (row(b, h, c), g_off + h)),
            pl.BlockSpec((1, RET_DV), lambda b, h, c, s: (0, h)),
            pl.BlockSpec((1, RET_CHUNK, RET_CHUNK), lambda b, h, c, s: (h, 0, 0)),
            pl.BlockSpec((1, RET_CHUNK, 1), lambda b, h, c, s: (h, 0, 0)),
            pl.BlockSpec((1, RET_CHUNK, 1), lambda b, h, c, s: (h, 0, 0)),
        ],
        out_specs=[
            pl.BlockSpec((RET_CHUNK, RET_DV), lambda b, h, c, s: (row(b, h, c), h)),
            pl.BlockSpec((1, 1, RET_DK, RET_DV), lambda b, h, c, s: (b, h, 0, 0)),
        ],
    )
    return pl.pallas_call(
        _retention_kernel,
        grid_spec=grid_spec,
        out_shape=[
            jax.ShapeDtypeStruct((batch * seq, RET_HEADS * RET_DV), BF16),
            jax.ShapeDtypeStruct((batch, RET_HEADS, RET_DK, RET_DV), F32),
        ],
        compiler_params=_params(3),
        name="retention_prompt",
    )(s_decay, proj, proj, proj, proj, gn_gain.reshape(1, -1), decay, q_decay, k_decay)


def _retention_step_kernel(gam_ref, q_ref, k_ref, v_ref, g_ref, gn_ref, s_ref, o_ref, sn_ref):
    gam = gam_ref[pl.program_id(1)]
    qc = q_ref[0, 0]
    kc = k_ref[0, 0] * (RET_DK ** -0.5)
    v = v_ref[0, 0]
    state = s_ref[0, 0]
    qk = jnp.sum(qc * kc, axis=0, keepdims=True)
    o = qk * v + jnp.sum((qc * gam) * state, axis=0, keepdims=True)
    sn_ref[0, 0] = gam * state + kc * v
    o_ref[0, 0] = _group_norm_gate(o, g_ref[0, 0], gn_ref[0]).astype(o_ref.dtype)


def _retention_sample(proj, gn_gain, state):
    b = proj.shape[0]
    qw = RET_HEADS * RET_DK
    vw = RET_HEADS * RET_DV
    q = proj[:, :qw].reshape(b, RET_HEADS, RET_DK, 1)
    k = proj[:, qw:2 * qw].reshape(b, RET_HEADS, RET_DK, 1)
    v = proj[:, 2 * qw:2 * qw + vw].reshape(b, RET_HEADS, 1, RET_DV)
    g = proj[:, 2 * qw + vw:].reshape(b, RET_HEADS, 1, RET_DV)
    gamma = jnp.exp(_retention_log_decay())
    col = pl.BlockSpec((1, 1, RET_DK, 1), lambda i, h, s: (i, h, 0, 0))
    rowv = pl.BlockSpec((1, 1, 1, RET_DV), lambda i, h, s: (i, h, 0, 0))
    st = pl.BlockSpec((1, 1, RET_DK, RET_DV), lambda i, h, s: (i, h, 0, 0))
    grid_spec = pltpu.PrefetchScalarGridSpec(
        num_scalar_prefetch=1,
        grid=(b, RET_HEADS),
        in_specs=[col, col, rowv, rowv, pl.BlockSpec((1, 1, RET_DV), lambda i, h, s: (h, 0, 0)), st],
        out_specs=[rowv, st],
    )
    o, s_new = pl.pallas_call(
        _retention_step_kernel,
        grid_spec=grid_spec,
        out_shape=[
            jax.ShapeDtypeStruct((b, RET_HEADS, 1, RET_DV), BF16),
            jax.ShapeDtypeStruct(state.shape, F32),
        ],
        compiler_params=_params(2),
        name="retention_sample",
    )(gamma, q, k, v, g, gn_gain.reshape(RET_HEADS, 1, RET_DV), state)
    return o.reshape(b, vw), s_new


def _alibi_slopes():
    h = np.arange(1, ATT_Q_HEADS + 1, dtype=np.float32)
    return jnp.asarray(np.exp2(-8.0 * h / ATT_Q_HEADS).astype(np.float32))


def _segment_ones(width):
    r = lax.broadcasted_iota(jnp.int32, (width, width), 0) // ATT_HEAD_DIM
    c = lax.broadcasted_iota(jnp.int32, (width, width), 1) // ATT_HEAD_DIM
    return jnp.where(r == c, 1.0, 0.0).astype(BF16)


def _head_rmsnorm(x, gain, seg_ones):
    sq = x * x
    hi = sq.astype(BF16)
    lo = (sq - hi.astype(F32)).astype(BF16)
    ss = jnp.dot(hi, seg_ones, preferred_element_type=F32) + jnp.dot(lo, seg_ones, preferred_element_type=F32)
    return x * lax.rsqrt(ss * (1.0 / ATT_HEAD_DIM) + RMS_EPS) * gain


def _sink_softmax(logits, sink):
    m = jnp.maximum(jnp.max(logits, axis=-1, keepdims=True), sink)
    e = jnp.exp(logits - m)
    denom = jnp.sum(e, axis=-1, keepdims=True) + jnp.exp(sink - m)
    return e / denom


def _attn_kernel(slope_ref, sink_ref, q_ref, k_ref, v_ref, qg_ref, kg_ref, o_ref, kc_ref, vc_ref, kprev, vprev):
    pair = pl.program_id(1)
    n = pl.program_id(2)

    @pl.when(n == 0)
    def _():
        kprev[...] = jnp.zeros_like(kprev)
        vprev[...] = jnp.zeros_like(vprev)

    seg_ones = _segment_ones(LANES)
    kn = _head_rmsnorm(k_ref[...], kg_ref[...], seg_ones)
    vc = v_ref[...]
    kk = jnp.concatenate([kprev[...], kn], axis=0)
    vv = jnp.concatenate([vprev[...], vc], axis=0)
    kk_sw = pltpu.roll(kk, ATT_HEAD_DIM, axis=1)
    vv_sw = pltpu.roll(vv, ATT_HEAD_DIM, axis=1)
    low = lax.broadcasted_iota(jnp.int32, (1, LANES), 1) < ATT_HEAD_DIM

    qi = lax.broadcasted_iota(jnp.int32, (ATT_BLOCK, 2 * ATT_BLOCK), 0)
    kj = lax.broadcasted_iota(jnp.int32, (ATT_BLOCK, 2 * ATT_BLOCK), 1)
    dist = ATT_BLOCK + qi - kj
    valid = (dist >= 0) & (dist <= WINDOW) & ((n > 0) | (kj >= ATT_BLOCK))
    distf = dist.astype(F32)

    vregs_per_head = ATT_GROUP * ATT_HEAD_DIM // LANES
    for p in range(2):
        k_here, k_other = (kk, kk_sw) if p == 0 else (kk_sw, kk)
        v_here, v_other = (vv, vv_sw) if p == 0 else (vv_sw, vv)
        k_half = (jnp.where(low, k_here, 0.0).astype(BF16), jnp.where(low, 0.0, k_other).astype(BF16))
        v_half = (jnp.where(low, v_here, 0.0).astype(BF16), jnp.where(low, 0.0, v_other).astype(BF16))
        for j in range(vregs_per_head):
            c0 = (p * vregs_per_head + j) * LANES
            qn = _head_rmsnorm(q_ref[:, c0:c0 + LANES], qg_ref[...], seg_ones) * (ATT_HEAD_DIM ** -0.5)
            qb = qn.astype(BF16)
            acc = jnp.zeros((ATT_BLOCK, LANES), F32)
            for e in range(2):
                hq = (2 * pair + p) * ATT_GROUP + 2 * j + e
                s = lax.dot_general(qb, k_half[e], (((1,), (1,)), ((), ())), preferred_element_type=F32)
                logits = jnp.where(valid, s - slope_ref[hq] * distf, -jnp.inf)
                probs = _sink_softmax(logits, sink_ref[hq])
                acc = acc + jnp.dot(probs.astype(BF16), v_half[e], preferred_element_type=F32)
            o_ref[:, c0:c0 + LANES] = acc.astype(o_ref.dtype)

    kprev[...] = kn
    vprev[...] = vc
    kc_ref[0] = kn
    vc_ref[0] = vc


def _attention_prompt(qkv, q_norm, k_norm, sinks, batch, seq):
    nb = seq // ATT_BLOCK
    pairs = ATT_KV_HEADS // 2
    qw = 2 * ATT_GROUP * ATT_HEAD_DIM
    k_off = ATT_Q_HEADS * ATT_HEAD_DIM // LANES
    v_off = k_off + pairs
    rep = LANES // ATT_HEAD_DIM
    grid_spec = pltpu.PrefetchScalarGridSpec(
        num_scalar_prefetch=2,
        grid=(batch, pairs, nb),
        in_specs=[
            pl.BlockSpec((ATT_BLOCK, qw), lambda b, i, n, *_: (b * nb + n, i)),
            pl.BlockSpec((ATT_BLOCK, LANES), lambda b, i, n, *_: (b * nb + n, k_off + i)),
            pl.BlockSpec((ATT_BLOCK, LANES), lambda b, i, n, *_: (b * nb + n, v_off + i)),
            pl.BlockSpec((1, LANES), lambda b, i, n, *_: (0, 0)),
            pl.BlockSpec((1, LANES), lambda b, i, n, *_: (0, 0)),
        ],
        out_specs=[
            pl.BlockSpec((ATT_BLOCK, qw), lambda b, i, n, *_: (b * nb + n, i)),
            pl.BlockSpec((1, ATT_BLOCK, LANES), lambda b, i, n, *_: (b, 0, i)),
            pl.BlockSpec((1, ATT_BLOCK, LANES), lambda b, i, n, *_: (b, 0, i)),
        ],
        scratch_shapes=[pltpu.VMEM((ATT_BLOCK, LANES), F32), pltpu.VMEM((ATT_BLOCK, LANES), F32)],
    )
    kvw = ATT_KV_HEADS * ATT_HEAD_DIM
    return pl.pallas_call(
        _attn_kernel,
        grid_spec=grid_spec,
        out_shape=[
            jax.ShapeDtypeStruct((batch * seq, ATT_Q_HEADS * ATT_HEAD_DIM), BF16),
            jax.ShapeDtypeStruct((batch, ATT_BLOCK, kvw), F32),
            jax.ShapeDtypeStruct((batch, ATT_BLOCK, kvw), F32),
        ],
        compiler_params=_params(3),
        name="attention_prompt",
    )(_alibi_slopes(), sinks, qkv, qkv, qkv, jnp.tile(q_norm, rep).reshape(1, LANES),
      jnp.tile(k_norm, rep).reshape(1, LANES))


def _attn_step_kernel(q_ref, k_ref, v_ref, ck_ref, cv_ref, qg_ref, kg_ref, slope_ref, sink_ref, o_ref, kn_ref):
    kvw = ATT_KV_HEADS * ATT_HEAD_DIM
    rows = ATT_KV_HEADS * ATT_GROUP
    pad = 8
    seg_ones = _segment_ones(kvw)
    kn = _head_rmsnorm(k_ref[0], kg_ref[...], seg_ones)
    kn_ref[0] = kn
    qn = _head_rmsnorm(q_ref[0], qg_ref[...], seg_ones) * (ATT_HEAD_DIM ** -0.5)
    r_head = lax.broadcasted_iota(jnp.int32, (rows, kvw), 0) // ATT_GROUP
    c_head = lax.broadcasted_iota(jnp.int32, (rows, kvw), 1) // ATT_HEAD_DIM
    own = r_head == c_head
    qm = jnp.where(own, jnp.concatenate([qn] * ATT_KV_HEADS, axis=0), 0.0).astype(BF16)
    wb = ck_ref.shape[1]
    first = lax.broadcasted_iota(jnp.int32, (pad, kvw), 0) == 0
    kk = jnp.concatenate([ck_ref[0], jnp.where(first, jnp.broadcast_to(kn, (pad, kvw)), 0.0)], axis=0)
    vv = jnp.concatenate([cv_ref[0], jnp.where(first, jnp.broadcast_to(v_ref[0], (pad, kvw)), 0.0)], axis=0)
    s = lax.dot_general(qm, kk.astype(BF16), (((1,), (1,)), ((), ())), preferred_element_type=F32)
    col = lax.broadcasted_iota(jnp.int32, (1, wb + pad), 1)
    dist = jnp.where(col < wb, wb - col, 0)
    valid = (col <= wb) & (dist <= WINDOW)
    logits = jnp.where(valid, s - slope_ref[...] * dist.astype(F32), -jnp.inf)
    probs = _sink_softmax(logits, sink_ref[...])
    full = jnp.dot(probs.astype(BF16), vv.astype(BF16), preferred_element_type=F32)
    full = jnp.where(own, full, 0.0)
    out = full[0:ATT_GROUP]
    for h in range(1, ATT_KV_HEADS):
        out = out + full[h * ATT_GROUP:(h + 1) * ATT_GROUP]
    o_ref[0] = out.astype(o_ref.dtype)


def _attention_sample(qkv, cache_k, cache_v, q_norm, k_norm, sinks):
    b = qkv.shape[0]
    wb = cache_k.shape[1]
    kvw = ATT_KV_HEADS * ATT_HEAD_DIM
    qw = ATT_Q_HEADS * ATT_HEAD_DIM
    rows = ATT_KV_HEADS * ATT_GROUP
    q = qkv[:, :qw].reshape(b, ATT_KV_HEADS, ATT_GROUP, ATT_HEAD_DIM).transpose(0, 2, 1, 3).reshape(b, ATT_GROUP, kvw)
    k = qkv[:, qw:qw + kvw].reshape(b, 1, kvw)
    v = qkv[:, qw + kvw:].reshape(b, 1, kvw)
    ck = cache_k.reshape(b, wb, kvw)
    cv = cache_v.reshape(b, wb, kvw)
    per_b = lambda shape: pl.BlockSpec((1,) + shape, lambda i: (i, 0, 0))
    const = lambda shape: pl.BlockSpec(shape, lambda i: (0, 0))
    o, kn = pl.pallas_call(
        _attn_step_kernel,
        grid=(b,),
        in_specs=[per_b((ATT_GROUP, kvw)), per_b((1, kvw)), per_b((1, kvw)), per_b((wb, kvw)), per_b((wb, kvw)),
                  const((1, kvw)), const((1, kvw)), const((rows, 1)), const((rows, 1))],
        out_specs=[per_b((ATT_GROUP, kvw)), per_b((1, kvw))],
        out_shape=[jax.ShapeDtypeStruct((b, ATT_GROUP, kvw), BF16), jax.ShapeDtypeStruct((b, 1, kvw), F32)],
        compiler_params=_params(1),
        name="attention_sample",
    )(q, k, v, ck, cv, jnp.tile(q_norm, ATT_KV_HEADS).reshape(1, kvw), jnp.tile(k_norm, ATT_KV_HEADS).reshape(1, kvw),
      _alibi_slopes().reshape(rows, 1), sinks.reshape(rows, 1))
    out = o.reshape(b, ATT_GROUP, ATT_KV_HEADS, ATT_HEAD_DIM).transpose(0, 2, 1, 3).reshape(b, qw)
    new_k = jnp.concatenate([ck, kn], axis=1)[:, -wb:].reshape(b, wb, ATT_KV_HEADS, ATT_HEAD_DIM)
    new_v = jnp.concatenate([cv, v], axis=1)[:, -wb:].reshape(b, wb, ATT_KV_HEADS, ATT_HEAD_DIM)
    return out, new_k, new_v


def _channel_and_embed(r, p, i, norm_mlp, w_up, w_down, norm_ple, w_gate, w_proj):
    u = _matmul(_rmsnorm(r, norm_mlp[i]), w_up, i, epilogue="relu2", out_dtype=BF16)
    r = _matmul(u, w_down, i, epilogue="residual", residual=r)
    return _matmul(_rmsnorm(r, norm_ple[i]), w_gate, i, epilogue="ple", residual=r, ple=(p, w_proj))


def kernel(x_prompt, x_sample, p_prompt, p_sample, state_ret, cache_k, cache_v, norm_mix, norm_mlp, norm_ple,
           ret_w_in, ret_gn_gain, ret_w_out, attn_w_qkv, attn_b_qkv, attn_q_norm, attn_k_norm, attn_sinks,
           attn_w_out, mlp_w_up, mlp_w_down, ple_w_gate, ple_w_proj):
    batch, seq, d = x_prompt.shape
    dec_batch, dec_seq, _ = x_sample.shape
    assert dec_seq == 1 and d == D_MODEL
    depth = norm_mix.shape[0]
    rp = x_prompt.reshape(batch * seq, d)
    rs = x_sample.reshape(dec_batch, d)
    pp = p_prompt.reshape(depth, batch * seq, -1)
    ps = p_sample.reshape(depth, dec_batch, -1)
    ret_p, ret_s, kbuf_p, vbuf_p, kbuf_s, vbuf_s = [], [], [], [], [], []
    for i in range(depth):
        j = i // N_MIXERS
        hp = _rmsnorm(rp, norm_mix[i])
        hs = _rmsnorm(rs, norm_mix[i])
        if i % N_MIXERS == 0:
            op, sp = _retention_prompt(_matmul(hp, ret_w_in, j), ret_gn_gain[j], batch, seq)
            os_, ss = _retention_sample(_matmul(hs, ret_w_in, j), ret_gn_gain[j], state_ret[j])
            rp = _matmul(op, ret_w_out, j, epilogue="residual", residual=rp)
            rs = _matmul(os_, ret_w_out, j, epilogue="residual", residual=rs)
            ret_p.append(sp)
            ret_s.append(ss)
        else:
            qkv_p = _matmul(hp, attn_w_qkv, j, epilogue="bias", bias=attn_b_qkv[j])
            op, kc, vc = _attention_prompt(qkv_p, attn_q_norm[j], attn_k_norm[j], attn_sinks[j], batch, seq)
            qkv_s = _matmul(hs, attn_w_qkv, j, epilogue="bias", bias=attn_b_qkv[j])
            os_, kb, vb = _attention_sample(qkv_s, cache_k[j], cache_v[j], attn_q_norm[j], attn_k_norm[j],
                                            attn_sinks[j])
            rp = _matmul(op, attn_w_out, j, epilogue="residual", residual=rp)
            rs = _matmul(os_, attn_w_out, j, epilogue="residual", residual=rs)
            kbuf_p.append(kc.reshape(batch, ATT_BLOCK, ATT_KV_HEADS, ATT_HEAD_DIM))
            vbuf_p.append(vc.reshape(batch, ATT_BLOCK, ATT_KV_HEADS, ATT_HEAD_DIM))
            kbuf_s.append(kb)
            vbuf_s.append(vb)
        rp = _channel_and_embed(rp, pp, i, norm_mlp, mlp_w_up, mlp_w_down, norm_ple, ple_w_gate, ple_w_proj)
        rs = _channel_and_embed(rs, ps, i, norm_mlp, mlp_w_up, mlp_w_down, norm_ple, ple_w_gate, ple_w_proj)
    return (rp.reshape(batch, seq, d), rs.reshape(dec_batch, dec_seq, d), jnp.stack(ret_p), jnp.stack(ret_s),
            jnp.stack(kbuf_p), jnp.stack(vbuf_p), jnp.stack(kbuf_s), jnp.stack(vbuf_s))
```

```python
import functools

import jax
import jax.numpy as jnp
import numpy as np
from jax import lax
from jax.experimental import pallas as pl
from jax.experimental.pallas import tpu as pltpu

D_MODEL = 4096
N_MIXERS = 2
RET_HEADS = 16
RET_DK = D_MODEL // RET_HEADS
RET_DV = 2 * D_MODEL // RET_HEADS
RET_CHUNK = 128
ATT_HEAD_DIM = 64
ATT_Q_HEADS = D_MODEL // ATT_HEAD_DIM
ATT_KV_HEADS = 8
ATT_GROUP = ATT_Q_HEADS // ATT_KV_HEADS
WINDOW = 128
ATT_BLOCK = 128
PAST_LEN = 16384
RMS_EPS = 1e-6
GN_EPS = 1e-6

LANES = 128
V7X_VMEM_LIMIT_BYTES = 56 * 1024 * 1024

BF16 = jnp.bfloat16
F32 = jnp.float32


def _params(n_grid_axes):
    return pltpu.CompilerParams(
        dimension_semantics=("arbitrary",) * n_grid_axes,
        vmem_limit_bytes=V7X_VMEM_LIMIT_BYTES,
    )


def _rmsnorm_kernel(x_ref, g_ref, o_ref):
    x = x_ref[...]
    y = x * lax.rsqrt(jnp.mean(x * x, axis=-1, keepdims=True) + RMS_EPS)
    o_ref[...] = (y * g_ref[...]).astype(o_ref.dtype)


def _rmsnorm(x, gain):
    m, d = x.shape
    tm = min(m, 256)
    return pl.pallas_call(
        _rmsnorm_kernel,
        grid=(m // tm,),
        in_specs=[pl.BlockSpec((tm, d), lambda i: (i, 0)), pl.BlockSpec((1, d), lambda i: (0, 0))],
        out_specs=pl.BlockSpec((tm, d), lambda i: (i, 0)),
        out_shape=jax.ShapeDtypeStruct((m, d), BF16),
        compiler_params=_params(1),
        name="rmsnorm",
    )(x, gain.reshape(1, d))


def _mm_tiles(m, k, n):
    if m <= 64:
        return m, min(n, 1024), min(k, 2048)
    if k <= 4096:
        return min(m, 1024), min(n, 512), k
    return min(m, 1024), min(n, 1024), 2048


def _mm_kernel(*refs, nk, epilogue):
    x_ref, w_ref = refs[0], refs[1]
    extra = refs[2:-1] if nk == 1 else refs[2:-2]
    o_ref = refs[-1] if nk == 1 else refs[-2]

    part = jnp.dot(x_ref[...], w_ref[...].astype(BF16), preferred_element_type=F32)

    def finish(acc):
        if epilogue == "plain":
            res = acc
        elif epilogue == "bias":
            res = acc + extra[0][...]
        elif epilogue == "residual":
            res = extra[0][...] + acc
        elif epilogue == "relu2":
            res = jnp.square(jnp.maximum(acc, 0.0))
        elif epilogue == "ple":
            r_ref, p_ref, wp_ref = extra
            proj = jnp.dot(p_ref[...].astype(BF16), wp_ref[...].astype(BF16), preferred_element_type=F32)
            res = r_ref[...] + jax.nn.sigmoid(acc) * proj
        else:
            raise ValueError(epilogue)
        o_ref[...] = res.astype(o_ref.dtype)

    if nk == 1:
        finish(part)
        return

    acc_ref = refs[-1]
    k = pl.program_id(2)

    @pl.when(k == 0)
    def _():
        acc_ref[...] = part

    @pl.when(jnp.logical_and(k > 0, k < nk - 1))
    def _():
        acc_ref[...] += part

    @pl.when(k == nk - 1)
    def _():
        finish(acc_ref[...] + part)


def _matmul(x, w, layer, *, epilogue="plain", out_dtype=F32, bias=None, residual=None, ple=None):
    m, k = x.shape
    n = w.shape[2]
    tm, tn, tk = _mm_tiles(m, k, n)
    nk = k // tk
    in_specs = [
        pl.BlockSpec((tm, tk), lambda j, i, kk: (i, kk)),
        pl.BlockSpec((None, tk, tn), lambda j, i, kk: (layer, kk, j)),
    ]
    args = [x, w]
    if epilogue == "bias":
        in_specs.append(pl.BlockSpec((1, tn), lambda j, i, kk: (0, j)))
        args.append(bias.reshape(1, n))
    elif epilogue == "residual":
        in_specs.append(pl.BlockSpec((tm, tn), lambda j, i, kk: (i, j)))
        args.append(residual)
    elif epilogue == "ple":
        p, wp = ple
        pd = p.shape[2]
        in_specs += [
            pl.BlockSpec((tm, tn), lambda j, i, kk: (i, j)),
            pl.BlockSpec((None, tm, pd), lambda j, i, kk: (layer, i, 0)),
            pl.BlockSpec((None, pd, tn), lambda j, i, kk: (layer, 0, j)),
        ]
        args += [residual, p, wp]
    scratch = [] if nk == 1 else [pltpu.VMEM((tm, tn), F32)]
    return pl.pallas_call(
        functools.partial(_mm_kernel, nk=nk, epilogue=epilogue),
        grid=(n // tn, m // tm, nk),
        in_specs=in_specs,
        out_specs=pl.BlockSpec((tm, tn), lambda j, i, kk: (i, j)),
        out_shape=jax.ShapeDtypeStruct((m, n), out_dtype),
        scratch_shapes=scratch,
        compiler_params=_params(3),
        name="matmul_" + epilogue,
    )(*args)


def _retention_log_decay():
    return jnp.log1p(-jnp.exp2(-5.0 - jnp.arange(RET_HEADS, dtype=F32)))


def _group_norm_gate(o, g, gn):
    mu = jnp.mean(o, axis=-1, keepdims=True)
    d = o - mu
    var = jnp.mean(d * d, axis=-1, keepdims=True)
    return jax.nn.silu(g) * (d * lax.rsqrt(var + GN_EPS) * gn)


RET_HEADS_PER_STEP = 4
RET_ROWS_PER_STEP = 8


def _retention_kernel(sdec_ref, q_ref, k_ref, v_ref, g_ref, gn_ref, dec_ref, qd_ref, kd_ref, o_ref, s_ref):
    hg = pl.program_id(1)

    @pl.when(pl.program_id(2) == 0)
    def _():
        s_ref[...] = jnp.zeros_like(s_ref)

    for t in range(RET_HEADS_PER_STEP):
        qs = slice(t * RET_DK, (t + 1) * RET_DK)
        vs = slice(t * RET_DV, (t + 1) * RET_DV)
        q = q_ref[:, qs]
        k = k_ref[:, qs] * (RET_DK ** -0.5)
        vb = v_ref[:, vs].astype(BF16)
        state = s_ref[0, t]
        scores = lax.dot_general(q.astype(BF16), k.astype(BF16), (((1,), (1,)), ((), ())),
                                 preferred_element_type=F32) * dec_ref[t]
        o = jnp.dot(scores.astype(BF16), vb, preferred_element_type=F32)
        o = o + jnp.dot((q * qd_ref[t]).astype(BF16), state.astype(BF16), preferred_element_type=F32)
        kv = lax.dot_general((k * kd_ref[t]).astype(BF16), vb, (((0,), (0,)), ((), ())),
                             preferred_element_type=F32)
        s_ref[0, t] = sdec_ref[hg * RET_HEADS_PER_STEP + t] * state + kv
        o_ref[:, vs] = _group_norm_gate(o, g_ref[:, vs], gn_ref[:, vs]).astype(o_ref.dtype)


def _retention_prompt(proj, gn_gain, batch, seq):
    nc = seq // RET_CHUNK
    hps = RET_HEADS_PER_STEP
    groups = RET_HEADS // hps
    log_g = _retention_log_decay()
    idx = jnp.arange(RET_CHUNK, dtype=F32)
    diff = idx[:, None] - idx[None, :]
    causal = diff >= 0
    decay = jnp.where(causal[None], jnp.exp(jnp.where(causal, diff, 0.0)[None] * log_g[:, None, None]), 0.0)
    q_decay = jnp.exp((idx + 1.0)[None, :] * log_g[:, None])[:, :, None]
    k_decay = jnp.exp((RET_CHUNK - 1.0 - idx)[None, :] * log_g[:, None])[:, :, None]
    s_decay = jnp.exp(RET_CHUNK * log_g)
    qkw, vw = hps * RET_DK, hps * RET_DV
    k_off = RET_HEADS * RET_DK // qkw
    v_off = 2 * RET_HEADS * RET_DK // vw
    g_off = v_off + RET_HEADS * RET_DV // vw
    grid_spec = pltpu.PrefetchScalarGridSpec(
        num_scalar_prefetch=1,
        grid=(batch, groups, nc),
        in_specs=[
            pl.BlockSpec((RET_CHUNK, qkw), lambda b, h, c, s: (b * nc + c, h)),
            pl.BlockSpec((RET_CHUNK, qkw), lambda b, h, c, s: (b * nc + c, k_off + h)),
            pl.BlockSpec((RET_CHUNK, vw), lambda b, h, c, s: (b * nc + c, v_off + h)),
            pl.BlockSpec((RET_CHUNK, vw), lambda b, h, c, s: (b * nc + c, g_off + h)),
            pl.BlockSpec((1, vw), lambda b, h, c, s: (0, h)),
            pl.BlockSpec((hps, RET_CHUNK, RET_CHUNK), lambda b, h, c, s: (h, 0, 0)),
            pl.BlockSpec((hps, RET_CHUNK, 1), lambda b, h, c, s: (h, 0, 0)),
            pl.BlockSpec((hps, RET_CHUNK, 1), lambda b, h, c, s: (h, 0, 0)),
        ],
        out_specs=[
            pl.BlockSpec((RET_CHUNK, vw), lambda b, h, c, s: (b * nc + c, h)),
            pl.BlockSpec((1, hps, RET_DK, RET_DV), lambda b, h, c, s: (b, h, 0, 0)),
        ],
    )
    return pl.pallas_call(
        _retention_kernel,
        grid_spec=grid_spec,
        out_shape=[
            jax.ShapeDtypeStruct((batch * seq, RET_HEADS * RET_DV), BF16),
            jax.ShapeDtypeStruct((batch, RET_HEADS, RET_DK, RET_DV), F32),
        ],
        compiler_params=_params(3),
        name="retention_prompt",
    )(s_decay, proj, proj, proj, proj, gn_gain.reshape(1, -1), decay, q_decay, k_decay)


def _retention_step_kernel(gam_ref, q_ref, k_ref, v_ref, g_ref, gn_ref, s_ref, o_ref, sn_ref):
    rows = RET_ROWS_PER_STEP
    gam = gam_ref[pl.program_id(1)]
    q = q_ref[...]
    k = k_ref[...] * (RET_DK ** -0.5)
    v = v_ref[...]
    qk = jnp.sum(q * k, axis=-1, keepdims=True)
    qd = (q * gam).astype(BF16)
    first_k = lax.broadcasted_iota(jnp.int32, (rows, RET_DK), 0) == 0
    first_v = lax.broadcasted_iota(jnp.int32, (rows, RET_DV), 0) == 0
    inter = []
    for r in range(rows):
        state = s_ref[r, 0]
        inter.append(jnp.dot(qd[r:r + 1], state.astype(BF16), preferred_element_type=F32))
        k8 = jnp.where(first_k, jnp.broadcast_to(k[r:r + 1], (rows, RET_DK)), 0.0).astype(BF16)
        v8 = jnp.where(first_v, jnp.broadcast_to(v[r:r + 1], (rows, RET_DV)), 0.0).astype(BF16)
        kv = lax.dot_general(k8, v8, (((0,), (0,)), ((), ())), preferred_element_type=F32)
        sn_ref[r, 0] = gam * state + kv
    o = qk * v + jnp.concatenate(inter, axis=0)
    o_ref[...] = _group_norm_gate(o, g_ref[...], gn_ref[...])


def _retention_sample(proj, gn_gain, state):
    b = proj.shape[0]
    rows = RET_ROWS_PER_STEP
    k_off = RET_HEADS
    v_off = 2 * RET_HEADS * RET_DK // RET_DV
    g_off = v_off + RET_HEADS
    gamma = jnp.exp(_retention_log_decay())
    st = pl.BlockSpec((rows, 1, RET_DK, RET_DV), lambda i, h, s: (i, h, 0, 0))
    grid_spec = pltpu.PrefetchScalarGridSpec(
        num_scalar_prefetch=1,
        grid=(b // rows, RET_HEADS),
        in_specs=[
            pl.BlockSpec((rows, RET_DK), lambda i, h, s: (i, h)),
            pl.BlockSpec((rows, RET_DK), lambda i, h, s: (i, k_off + h)),
            pl.BlockSpec((rows, RET_DV), lambda i, h, s: (i, v_off + h)),
            pl.BlockSpec((rows, RET_DV), lambda i, h, s: (i, g_off + h)),
            pl.BlockSpec((1, RET_DV), lambda i, h, s: (0, h)),
            st,
        ],
        out_specs=[pl.BlockSpec((rows, RET_DV), lambda i, h, s: (i, h)), st],
    )
    o, s_new = pl.pallas_call(
        _retention_step_kernel,
        grid_spec=grid_spec,
        out_shape=[
            jax.ShapeDtypeStruct((b, RET_HEADS * RET_DV), F32),
            jax.ShapeDtypeStruct(state.shape, F32),
        ],
        compiler_params=_params(2),
        name="retention_sample",
    )(gamma, proj, proj, proj, proj, gn_gain.reshape(1, -1), state)
    return o.astype(BF16), s_new


def _alibi_slopes():
    h = np.arange(1, ATT_Q_HEADS + 1, dtype=np.float32)
    return jnp.asarray(np.exp2(-8.0 * h / ATT_Q_HEADS).astype(np.float32))


def _segment_ones(width):
    r = lax.broadcasted_iota(jnp.int32, (width, width), 0) // ATT_HEAD_DIM
    c = lax.broadcasted_iota(jnp.int32, (width, width), 1) // ATT_HEAD_DIM
    return jnp.where(r == c, 1.0, 0.0).astype(BF16)


def _head_rmsnorm(x, gain, seg_ones):
    sq = x * x
    hi = sq.astype(BF16)
    lo = (sq - hi.astype(F32)).astype(BF16)
    ss = jnp.dot(hi, seg_ones, preferred_element_type=F32) + jnp.dot(lo, seg_ones, preferred_element_type=F32)
    return x * lax.rsqrt(ss * (1.0 / ATT_HEAD_DIM) + RMS_EPS) * gain


def _sink_softmax(logits, sink):
    m = jnp.maximum(jnp.max(logits, axis=-1, keepdims=True), sink)
    e = jnp.exp(logits - m)
    denom = jnp.sum(e, axis=-1, keepdims=True) + jnp.exp(sink - m)
    return e / denom


Q_TILES_PER_KV = ATT_GROUP * ATT_HEAD_DIM // LANES


def _attn_kernel(sink_ref, q_ref, k_ref, v_ref, bias_ref, qg_ref, kg_ref, o_ref, kc_ref, vc_ref, kprev, vprev):
    pair = pl.program_id(1)

    @pl.when(pl.program_id(2) == 0)
    def _():
        kprev[...] = jnp.zeros_like(kprev)
        vprev[...] = jnp.zeros_like(vprev)

    seg_ones = _segment_ones(LANES)
    kn = _head_rmsnorm(k_ref[...], kg_ref[...], seg_ones)
    vc = v_ref[...]
    kk = jnp.concatenate([kprev[...], kn], axis=0)
    vv = jnp.concatenate([vprev[...], vc], axis=0)
    kk_sw = pltpu.roll(kk, ATT_HEAD_DIM, axis=1)
    vv_sw = pltpu.roll(vv, ATT_HEAD_DIM, axis=1)
    low = lax.broadcasted_iota(jnp.int32, (1, LANES), 1) < ATT_HEAD_DIM

    for p in range(2):
        k_here, k_other = (kk, kk_sw) if p == 0 else (kk_sw, kk)
        v_here, v_other = (vv, vv_sw) if p == 0 else (vv_sw, vv)
        k_half = (jnp.where(low, k_here, 0.0).astype(BF16), jnp.where(low, 0.0, k_other).astype(BF16))
        v_half = (jnp.where(low, v_here, 0.0).astype(BF16), jnp.where(low, 0.0, v_other).astype(BF16))
        q4 = jnp.concatenate(
            [q_ref[:, (p * Q_TILES_PER_KV + j) * LANES:(p * Q_TILES_PER_KV + j + 1) * LANES]
             for j in range(Q_TILES_PER_KV)], axis=0)
        qb = (_head_rmsnorm(q4, qg_ref[...], seg_ones) * (ATT_HEAD_DIM ** -0.5)).astype(BF16)
        acc = None
        for e in range(2):
            s = lax.dot_general(qb, k_half[e], (((1,), (1,)), ((), ())), preferred_element_type=F32)
            probs = []
            for j in range(Q_TILES_PER_KV):
                rows = slice(j * ATT_BLOCK, (j + 1) * ATT_BLOCK)
                hq = (2 * pair + p) * ATT_GROUP + 2 * j + e
                probs.append(_sink_softmax(s[rows] + bias_ref[2 * p + e, rows], sink_ref[hq]).astype(BF16))
            part = jnp.dot(jnp.concatenate(probs, axis=0), v_half[e], preferred_element_type=F32)
            acc = part if acc is None else acc + part
        for j in range(Q_TILES_PER_KV):
            c0 = (p * Q_TILES_PER_KV + j) * LANES
            o_ref[:, c0:c0 + LANES] = acc[j * ATT_BLOCK:(j + 1) * ATT_BLOCK].astype(o_ref.dtype)

    kprev[...] = kn
    vprev[...] = vc
    kc_ref[0] = kn
    vc_ref[0] = vc


def _attn_bias_table():
    qi = np.arange(ATT_BLOCK)[:, None]
    kj = np.arange(2 * ATT_BLOCK)[None, :]
    dist = ATT_BLOCK + qi - kj
    in_window = (dist >= 0) & (dist <= WINDOW)
    valid = np.stack([in_window & (kj >= ATT_BLOCK), in_window])
    slopes = np.exp2(-8.0 * np.arange(1, ATT_Q_HEADS + 1, dtype=np.float32) / ATT_Q_HEADS).astype(np.float32)
    sl = slopes.reshape(ATT_KV_HEADS // 2, 2, Q_TILES_PER_KV, 2).transpose(0, 1, 3, 2)
    pen = -(sl[..., None, None] * dist.astype(np.float32))
    bias = np.where(valid[:, None, None, None, None], pen[None], -np.inf).astype(np.float32)
    return jnp.asarray(bias.reshape(2, ATT_KV_HEADS // 2, 4, Q_TILES_PER_KV * ATT_BLOCK, 2 * ATT_BLOCK))


def _attention_prompt(qkv, q_norm, k_norm, sinks, batch, seq):
    nb = seq // ATT_BLOCK
    pairs = ATT_KV_HEADS // 2
    qw = 2 * ATT_GROUP * ATT_HEAD_DIM
    k_off = ATT_Q_HEADS * ATT_HEAD_DIM // LANES
    v_off = k_off + pairs
    rep = LANES // ATT_HEAD_DIM
    grid_spec = pltpu.PrefetchScalarGridSpec(
        num_scalar_prefetch=1,
        grid=(batch, pairs, nb),
        in_specs=[
            pl.BlockSpec((ATT_BLOCK, qw), lambda b, i, n, *_: (b * nb + n, i)),
            pl.BlockSpec((ATT_BLOCK, LANES), lambda b, i, n, *_: (b * nb + n, k_off + i)),
            pl.BlockSpec((ATT_BLOCK, LANES), lambda b, i, n, *_: (b * nb + n, v_off + i)),
            pl.BlockSpec((None, None, 4, Q_TILES_PER_KV * ATT_BLOCK, 2 * ATT_BLOCK),
                         lambda b, i, n, *_: (jnp.minimum(n, 1), i, 0, 0, 0)),
            pl.BlockSpec((1, LANES), lambda b, i, n, *_: (0, 0)),
            pl.BlockSpec((1, LANES), lambda b, i, n, *_: (0, 0)),
        ],
        out_specs=[
            pl.BlockSpec((ATT_BLOCK, qw), lambda b, i, n, *_: (b * nb + n, i)),
            pl.BlockSpec((1, ATT_BLOCK, LANES), lambda b, i, n, *_: (b, 0, i)),
            pl.BlockSpec((1, ATT_BLOCK, LANES), lambda b, i, n, *_: (b, 0, i)),
        ],
        scratch_shapes=[pltpu.VMEM((ATT_BLOCK, LANES), F32), pltpu.VMEM((ATT_BLOCK, LANES), F32)],
    )
    kvw = ATT_KV_HEADS * ATT_HEAD_DIM
    return pl.pallas_call(
        _attn_kernel,
        grid_spec=grid_spec,
        out_shape=[
            jax.ShapeDtypeStruct((batch * seq, ATT_Q_HEADS * ATT_HEAD_DIM), BF16),
            jax.ShapeDtypeStruct((batch, ATT_BLOCK, kvw), F32),
            jax.ShapeDtypeStruct((batch, ATT_BLOCK, kvw), F32),
        ],
        compiler_params=_params(3),
        name="attention_prompt",
    )(sinks, qkv, qkv, qkv, _attn_bias_table(), jnp.tile(q_norm, rep).reshape(1, LANES),
      jnp.tile(k_norm, rep).reshape(1, LANES))


def _attn_step_kernel(q_ref, k_ref, v_ref, ck_ref, cv_ref, qg_ref, kg_ref, slope_ref, sink_ref, o_ref, kn_ref):
    kvw = ATT_KV_HEADS * ATT_HEAD_DIM
    rows = ATT_KV_HEADS * ATT_GROUP
    pad = 8
    seg_ones = _segment_ones(kvw)
    kn = _head_rmsnorm(k_ref[0], kg_ref[...], seg_ones)
    kn_ref[0] = kn
    qn = _head_rmsnorm(q_ref[0], qg_ref[...], seg_ones) * (ATT_HEAD_DIM ** -0.5)
    r_head = lax.broadcasted_iota(jnp.int32, (rows, kvw), 0) // ATT_GROUP
    c_head = lax.broadcasted_iota(jnp.int32, (rows, kvw), 1) // ATT_HEAD_DIM
    own = r_head == c_head
    qm = jnp.where(own, jnp.concatenate([qn] * ATT_KV_HEADS, axis=0), 0.0).astype(BF16)
    wb = ck_ref.shape[1]
    first = lax.broadcasted_iota(jnp.int32, (pad, kvw), 0) == 0
    kk = jnp.concatenate([ck_ref[0], jnp.where(first, jnp.broadcast_to(kn, (pad, kvw)), 0.0)], axis=0)
    vv = jnp.concatenate([cv_ref[0], jnp.where(first, jnp.broadcast_to(v_ref[0], (pad, kvw)), 0.0)], axis=0)
    s = lax.dot_general(qm, kk.astype(BF16), (((1,), (1,)), ((), ())), preferred_element_type=F32)
    col = lax.broadcasted_iota(jnp.int32, (1, wb + pad), 1)
    dist = jnp.where(col < wb, wb - col, 0)
    valid = (col <= wb) & (dist <= WINDOW)
    logits = jnp.where(valid, s - slope_ref[...] * dist.astype(F32), -jnp.inf)
    probs = _sink_softmax(logits, sink_ref[...])
    full = jnp.dot(probs.astype(BF16), vv.astype(BF16), preferred_element_type=F32)
    full = jnp.where(own, full, 0.0)
    out = full[0:ATT_GROUP]
    for h in range(1, ATT_KV_HEADS):
        out = out + full[h * ATT_GROUP:(h + 1) * ATT_GROUP]
    o_ref[0] = out.astype(o_ref.dtype)


def _attention_sample(qkv, cache_k, cache_v, q_norm, k_norm, sinks):
    b = qkv.shape[0]
    wb = cache_k.shape[1]
    kvw = ATT_KV_HEADS * ATT_HEAD_DIM
    qw = ATT_Q_HEADS * ATT_HEAD_DIM
    rows = ATT_KV_HEADS * ATT_GROUP
    q = qkv[:, :qw].reshape(b, ATT_KV_HEADS, ATT_GROUP, ATT_HEAD_DIM).transpose(0, 2, 1, 3).reshape(b, ATT_GROUP, kvw)
    k = qkv[:, qw:qw + kvw].reshape(b, 1, kvw)
    v = qkv[:, qw + kvw:].reshape(b, 1, kvw)
    ck = cache_k.reshape(b, wb, kvw)
    cv = cache_v.reshape(b, wb, kvw)
    per_b = lambda shape: pl.BlockSpec((1,) + shape, lambda i: (i, 0, 0))
    const = lambda shape: pl.BlockSpec(shape, lambda i: (0, 0))
    o, kn = pl.pallas_call(
        _attn_step_kernel,
        grid=(b,),
        in_specs=[per_b((ATT_GROUP, kvw)), per_b((1, kvw)), per_b((1, kvw)), per_b((wb, kvw)), per_b((wb, kvw)),
                  const((1, kvw)), const((1, kvw)), const((rows, 1)), const((rows, 1))],
        out_specs=[per_b((ATT_GROUP, kvw)), per_b((1, kvw))],
        out_shape=[jax.ShapeDtypeStruct((b, ATT_GROUP, kvw), BF16), jax.ShapeDtypeStruct((b, 1, kvw), F32)],
        compiler_params=_params(1),
        name="attention_sample",
    )(q, k, v, ck, cv, jnp.tile(q_norm, ATT_KV_HEADS).reshape(1, kvw), jnp.tile(k_norm, ATT_KV_HEADS).reshape(1, kvw),
      _alibi_slopes().reshape(rows, 1), sinks.reshape(rows, 1))
    out = o.reshape(b, ATT_GROUP, ATT_KV_HEADS, ATT_HEAD_DIM).transpose(0, 2, 1, 3).reshape(b, qw)
    new_k = jnp.concatenate([ck, kn], axis=1)[:, -wb:].reshape(b, wb, ATT_KV_HEADS, ATT_HEAD_DIM)
    new_v = jnp.concatenate([cv, v], axis=1)[:, -wb:].reshape(b, wb, ATT_KV_HEADS, ATT_HEAD_DIM)
    return out, new_k, new_v


def _channel_and_embed(r, p, i, norm_mlp, w_up, w_down, norm_ple, w_gate, w_proj):
    u = _matmul(_rmsnorm(r, norm_mlp[i]), w_up, i, epilogue="relu2", out_dtype=BF16)
    r = _matmul(u, w_down, i, epilogue="residual", residual=r)
    return _matmul(_rmsnorm(r, norm_ple[i]), w_gate, i, epilogue="ple", residual=r, ple=(p, w_proj))


def kernel(x_prompt, x_sample, p_prompt, p_sample, state_ret, cache_k, cache_v, norm_mix, norm_mlp, norm_ple,
           ret_w_in, ret_gn_gain, ret_w_out, attn_w_qkv, attn_b_qkv, attn_q_norm, attn_k_norm, attn_sinks,
           attn_w_out, mlp_w_up, mlp_w_down, ple_w_gate, ple_w_proj):
    batch, seq, d = x_prompt.shape
    dec_batch, dec_seq, _ = x_sample.shape
    assert dec_seq == 1 and d == D_MODEL
    depth = norm_mix.shape[0]
    rp = x_prompt.reshape(batch * seq, d)
    rs = x_sample.reshape(dec_batch, d)
    pp = p_prompt.reshape(depth, batch * seq, -1)
    ps = p_sample.reshape(depth, dec_batch, -1)
    ret_p, ret_s, kbuf_p, vbuf_p, kbuf_s, vbuf_s = [], [], [], [], [], []
    for i in range(depth):
        j = i // N_MIXERS
        hp = _rmsnorm(rp, norm_mix[i])
        hs = _rmsnorm(rs, norm_mix[i])
        if i % N_MIXERS == 0:
            op, sp = _retention_prompt(_matmul(hp, ret_w_in, j), ret_gn_gain[j], batch, seq)
            os_, ss = _retention_sample(_matmul(hs, ret_w_in, j), ret_gn_gain[j], state_ret[j])
            rp = _matmul(op, ret_w_out, j, epilogue="residual", residual=rp)
            rs = _matmul(os_, ret_w_out, j, epilogue="residual", residual=rs)
            ret_p.append(sp)
            ret_s.append(ss)
        else:
            qkv_p = _matmul(hp, attn_w_qkv, j, epilogue="bias", bias=attn_b_qkv[j])
            op, kc, vc = _attention_prompt(qkv_p, attn_q_norm[j], attn_k_norm[j], attn_sinks[j], batch, seq)
            qkv_s = _matmul(hs, attn_w_qkv, j, epilogue="bias", bias=attn_b_qkv[j])
            os_, kb, vb = _attention_sample(qkv_s, cache_k[j], cache_v[j], attn_q_norm[j], attn_k_norm[j],
                                            attn_sinks[j])
            rp = _matmul(op, attn_w_out, j, epilogue="residual", residual=rp)
            rs = _matmul(os_, attn_w_out, j, epilogue="residual", residual=rs)
            kbuf_p.append(kc.reshape(batch, ATT_BLOCK, ATT_KV_HEADS, ATT_HEAD_DIM))
            vbuf_p.append(vc.reshape(batch, ATT_BLOCK, ATT_KV_HEADS, ATT_HEAD_DIM))
            kbuf_s.append(kb)
            vbuf_s.append(vb)
        rp = _channel_and_embed(rp, pp, i, norm_mlp, mlp_w_up, mlp_w_down, norm_ple, ple_w_gate, ple_w_proj)
        rs = _channel_and_embed(rs, ps, i, norm_mlp, mlp_w_up, mlp_w_down, norm_ple, ple_w_gate, ple_w_proj)
    return (rp.reshape(batch, seq, d), rs.reshape(dec_batch, dec_seq, d), jnp.stack(ret_p), jnp.stack(ret_s),
            jnp.stack(kbuf_p), jnp.stack(vbuf_p), jnp.stack(kbuf_s), jnp.stack(vbuf_s))
```
